```python
import math
import jax
import jax.numpy as jnp
from jax import lax
import numpy as np

D_MODEL = 1024
BATCH = 8
SEQ = 2048
DEPTH = 1
DEC_BATCH = 128
DEC_SEQ = 8
PAST_LEN = 2048
PAGE_SIZE = 128

EPS = 1e-6
MEM_LEN = 256
N_BRANCH = 3
GLA_HEADS = 4
GLA_DK = 128
GLA_DV = 128
GLA_RANK = 16
GLA_TAU = 16.0
GLA_CHUNK = 64
FOX_HEADS = 8
FOX_HD = 64
FOX_QBLOCK = 128
FOX_FGATE_BIAS = 3.0
MEM_HEADS = 4
MEM_HD = 128
PEER_HEADS = 8
PEER_NKEYS = 128
PEER_NEXP = PEER_NKEYS * PEER_NKEYS
PEER_DKEY = 256
PEER_TOPK = 16
PEER_BLOCK = 128

GLA_W = GLA_HEADS * GLA_DK
GLA_VW = GLA_HEADS * GLA_DV
FOX_W = FOX_HEADS * FOX_HD
MEM_W = MEM_HEADS * MEM_HD
IN_WIDTHS = (GLA_W, GLA_W, GLA_VW, GLA_VW, GLA_RANK, FOX_W, FOX_W, FOX_W, FOX_HEADS, MEM_W, N_BRANCH * D_MODEL)
IN_W = sum(IN_WIDTHS)

kernel_name = 'hybrid_gla_fox_mem_peer_step'


def rmsnorm(x, g):
    xf = x.astype(jnp.float32)
    y = xf * lax.rsqrt(jnp.mean(xf * xf, axis=-1, keepdims=True) + EPS)
    return (y * g.astype(jnp.float32)).astype(x.dtype)


def project_in(n, lw):
    B, T, _ = n.shape
    z = n @ lw['w_in']
    parts = []
    off = 0
    for w in IN_WIDTHS:
        parts.append(z[..., off:off + w])
        off += w
    gq, gk, gv, gr, glr, fq, fk, fv, ff, mq, gt = parts
    q_gla = gq.reshape(B, T, GLA_HEADS, GLA_DK) * (GLA_DK ** -0.5)
    k_gla = gk.reshape(B, T, GLA_HEADS, GLA_DK)
    v_gla = gv.reshape(B, T, GLA_HEADS, GLA_DV)
    r_gla = jax.nn.silu(gr)
    la = jax.nn.log_sigmoid((glr @ lw['w_a2'] + lw['b_a2']).astype(jnp.float32)) / GLA_TAU
    la = la.reshape(B, T, GLA_HEADS, GLA_DK)
    q_fox = fq.reshape(B, T, FOX_HEADS, FOX_HD)
    k_fox = fk.reshape(B, T, FOX_HEADS, FOX_HD)
    v_fox = fv.reshape(B, T, FOX_HEADS, FOX_HD)
    logf = jax.nn.log_sigmoid((ff + lw['b_fgate']).astype(jnp.float32))
    q_mem = mq.reshape(B, T, MEM_HEADS, MEM_HD)
    gates = jax.nn.sigmoid(gt + lw['b_gate']).reshape(B, T, N_BRANCH, D_MODEL)
    return q_gla, k_gla, v_gla, r_gla, la, q_fox, k_fox, v_fox, logf, q_mem, gates


def gla_chunked(q, k, v, la, s0):
    B, T, H, dk = q.shape
    dv = v.shape[-1]
    c = math.gcd(T, GLA_CHUNK)
    nc = T // c

    def to_chunks(a):
        return a.reshape(B, nc, c, H, a.shape[-1]).transpose(1, 0, 3, 2, 4)

    mask = jnp.tril(jnp.ones((c, c), dtype=bool))[:, :, None]

    def step(s, inp):
        qc, kc, vc, lc = inp
        qf = qc.astype(jnp.float32)
        kf = kc.astype(jnp.float32)
        vf = vc.astype(jnp.float32)
        cum = jnp.cumsum(lc, axis=2)
        o_inter = jnp.einsum('bhtk,bhkv->bhtv', qf * jnp.exp(cum), s)
        diff = cum[:, :, :, None, :] - cum[:, :, None, :, :]
        decay = jnp.where(mask, jnp.exp(jnp.where(mask, diff, 0.0)), 0.0)
        att = jnp.einsum('bhtk,bhsk,bhtsk->bhts', qf, kf, decay)
        o_intra = jnp.einsum('bhts,bhsv->bhtv', att, vf)
        last = cum[:, :, -1:, :]
        s_new = jnp.exp(last[:, :, 0, :])[..., None] * s + jnp.einsum('bhsk,bhsv->bhkv', kf * jnp.exp(last - cum), vf)
        return s_new, o_inter + o_intra

    s_T, o = lax.scan(step, s0.astype(jnp.float32), (to_chunks(q), to_chunks(k), to_chunks(v), to_chunks(la)))
    o = o.transpose(1, 0, 3, 2, 4).reshape(B, T, H, dv)
    return o.astype(v.dtype), s_T.astype(s0.dtype)


def fox_attend(q, k, v, d_q, d_k, q_pos, k_pos):
    B, Tq, H, hd = q.shape
    qb = math.gcd(Tq, FOX_QBLOCK)
    nb = Tq // qb
    qs = q.reshape(B, nb, qb, H, hd).transpose(1, 0, 2, 3, 4)
    dqs = d_q.reshape(B, nb, qb, H).transpose(1, 0, 3, 2)
    ps = q_pos.reshape(nb, qb)
    dk_t = d_k.transpose(0, 2, 1)
    scale = FOX_HD ** -0.5

    def blk(args):
        qi, di, pi = args
        s = jnp.einsum('bqhd,bkhd->bhqk', qi, k).astype(jnp.float32) * scale
        s = s + (di[..., None] - dk_t[:, :, None, :])
        s = jnp.where(k_pos[None, :] <= pi[:, None], s, -jnp.inf)
        p = jax.nn.softmax(s, axis=-1)
        return jnp.einsum('bhqk,bkhd->bqhd', p.astype(v.dtype), v)

    o = lax.map(blk, (qs, dqs, ps))
    return o.transpose(1, 0, 2, 3, 4).reshape(B, Tq, H, hd)


def mem_kv(mem, g_mem, w_mem_kv):
    B, M, _ = mem.shape
    kv = rmsnorm(mem, g_mem) @ w_mem_kv
    mk = kv[..., :MEM_W].reshape(B, M, MEM_HEADS, MEM_HD)
    mv = kv[..., MEM_W:].reshape(B, M, MEM_HEADS, MEM_HD)
    return mk, mv


def mem_attend(q, mk, mv):
    s = jnp.einsum('bthd,bmhd->bhtm', q, mk).astype(jnp.float32) * (MEM_HD ** -0.5)
    p = jax.nn.softmax(s, axis=-1)
    return jnp.einsum('bhtm,bmhd->bthd', p.astype(mv.dtype), mv)


def merge_branches(o_gla, r_gla, o_fox, o_mem, gates, lw):
    B, T = o_gla.shape[:2]
    og = rmsnorm(o_gla, lw['g_gla_head']).reshape(B, T, GLA_VW) * r_gla
    m = (gates[:, :, 0] * (og @ lw['w_gla_o'])
         + gates[:, :, 1] * (o_fox.reshape(B, T, FOX_W) @ lw['w_fox_o'])
         + gates[:, :, 2] * (o_mem.reshape(B, T, MEM_W) @ lw['w_mem_o']))
    return m @ lw['w_out']


def peer_ffn(xn, w_pq, k1, k2, u_tab, v_tab):
    shp = xn.shape
    x2 = xn.reshape(-1, shp[-1])
    T = x2.shape[0]
    q = (x2 @ w_pq).reshape(T, PEER_HEADS, 2, PEER_DKEY // 2)
    s1 = jnp.einsum('thd,nd->thn', q[:, :, 0], k1).astype(jnp.float32)
    s2 = jnp.einsum('thd,nd->thn', q[:, :, 1], k2).astype(jnp.float32)
    v1, i1 = lax.top_k(s1, PEER_TOPK)
    v2, i2 = lax.top_k(s2, PEER_TOPK)
    cand = (v1[..., :, None] + v2[..., None, :]).reshape(T, PEER_HEADS, PEER_TOPK * PEER_TOPK)
    cidx = (i1[..., :, None] * PEER_NKEYS + i2[..., None, :]).reshape(T, PEER_HEADS, PEER_TOPK * PEER_TOPK)
    sc, pos = lax.top_k(cand, PEER_TOPK)
    eidx = jnp.take_along_axis(cidx, pos, axis=-1)
    g = jax.nn.softmax(sc, axis=-1)
    pad = (-T) % PEER_BLOCK
    nb = (T + pad) // PEER_BLOCK
    xp = jnp.pad(x2, ((0, pad), (0, 0))).reshape(nb, PEER_BLOCK, shp[-1])
    ip = jnp.pad(eidx, ((0, pad), (0, 0), (0, 0))).reshape(nb, PEER_BLOCK, PEER_HEADS, PEER_TOPK)
    gp = jnp.pad(g, ((0, pad), (0, 0), (0, 0))).reshape(nb, PEER_BLOCK, PEER_HEADS, PEER_TOPK)

    def blk(args):
        xb, ib, gb = args
        a = jax.nn.gelu(jnp.einsum('td,thkd->thk', xb, u_tab[ib]).astype(jnp.float32))
        return jnp.einsum('thk,thkd->td', (gb * a).astype(v_tab.dtype), v_tab[ib])

    out = lax.map(blk, (xp, ip, gp))
    return out.reshape(nb * PEER_BLOCK, shp[-1])[:T].reshape(shp).astype(xn.dtype)


def gather_pages(pool, page_table):
    g = pool[page_table]
    return g.reshape((page_table.shape[0], page_table.shape[1] * pool.shape[1]) + pool.shape[2:])


def hybrid_layer(x, mk, mv, gla_s0, past_k, past_v, past_logf, lw):
    B, T, _ = x.shape
    n = rmsnorm(x, lw['g_mix'])
    q_gla, k_gla, v_gla, r_gla, la, q_fox, k_fox, v_fox, logf, q_mem, gates = project_in(n, lw)
    o_gla, s_new = gla_chunked(q_gla, k_gla, v_gla, la, gla_s0)
    if past_k is None:
        k_all, v_all, lf_all = k_fox, v_fox, logf
        q_pos = jnp.arange(T)
    else:
        k_all = jnp.concatenate([past_k, k_fox], axis=1)
        v_all = jnp.concatenate([past_v, v_fox], axis=1)
        lf_all = jnp.concatenate([past_logf.astype(jnp.float32), logf], axis=1)
        q_pos = past_k.shape[1] + jnp.arange(T)
    k_pos = jnp.arange(k_all.shape[1])
    d_all = jnp.cumsum(lf_all, axis=1)
    o_fox = fox_attend(q_fox, k_all, v_all, d_all[:, -T:], d_all, q_pos, k_pos)
    o_mem = mem_attend(q_mem, mk, mv)
    h = x + merge_branches(o_gla, r_gla, o_fox, o_mem, gates, lw)
    h = h + peer_ffn(rmsnorm(h, lw['g_ffn']), lw['w_pq'], lw['peer_k1'], lw['peer_k2'], lw['peer_u'], lw['peer_v'])
    return h, k_fox, v_fox, logf, s_new


def setup_inputs(seed: int = 0) -> dict:
    key = jax.random.key(seed)
    k = jax.random.split(key, 30)
    n_pages = PAST_LEN // PAGE_SIZE
    n_used = DEC_BATCH * n_pages
    n_phys = n_used + n_used // 4

    def nrm(kk, shape, scale=1.0):
        return jax.random.normal(kk, shape, jnp.float32) * scale

    page_table = jax.random.permutation(k[8], n_phys)[:n_used].reshape(DEC_BATCH, n_pages).astype(jnp.int32)
    return {
        'x_prompt': nrm(k[0], (BATCH, SEQ, D_MODEL)),
        'x_sample': nrm(k[1], (DEC_BATCH, DEC_SEQ, D_MODEL)),
        'cache_fox_k': nrm(k[2], (DEPTH, n_phys, PAGE_SIZE, FOX_HEADS, FOX_HD)),
        'cache_fox_v': nrm(k[3], (DEPTH, n_phys, PAGE_SIZE, FOX_HEADS, FOX_HD)),
        'cache_fox_logf': jax.nn.log_sigmoid(nrm(k[4], (DEPTH, n_phys, PAGE_SIZE, FOX_HEADS)) + FOX_FGATE_BIAS),
        'state_gla': nrm(k[5], (DEPTH, DEC_BATCH, GLA_HEADS, GLA_DK, GLA_DV)),
        'cache_mem_k': nrm(k[6], (DEPTH, DEC_BATCH, MEM_LEN, MEM_HEADS, MEM_HD)),
        'cache_mem_v': nrm(k[7], (DEPTH, DEC_BATCH, MEM_LEN, MEM_HEADS, MEM_HD)),
        'page_table': page_table,
        'mem_prompt': nrm(k[9], (BATCH, MEM_LEN, D_MODEL)),
        'g_mix': 1.0 + nrm(k[10], (DEPTH, D_MODEL), 0.02),
        'w_in': nrm(k[11], (DEPTH, D_MODEL, IN_W), D_MODEL ** -0.5),
        'w_a2': nrm(k[12], (DEPTH, GLA_RANK, GLA_W), GLA_RANK ** -0.5),
        'b_a2': nrm(k[13], (DEPTH, GLA_W), 0.1),
        'b_fgate': FOX_FGATE_BIAS + nrm(k[14], (DEPTH, FOX_HEADS), 0.1),
        'b_gate': nrm(k[15], (DEPTH, N_BRANCH * D_MODEL), 0.02),
        'g_gla_head': 1.0 + nrm(k[16], (DEPTH, GLA_DV), 0.02),
        'w_gla_o': nrm(k[17], (DEPTH, GLA_VW, D_MODEL), GLA_VW ** -0.5),
        'w_fox_o': nrm(k[18], (DEPTH, FOX_W, D_MODEL), FOX_W ** -0.5),
        'w_mem_o': nrm(k[19], (DEPTH, MEM_W, D_MODEL), MEM_W ** -0.5),
        'w_out': nrm(k[20], (DEPTH, D_MODEL, D_MODEL), D_MODEL ** -0.5),
        'g_mem': 1.0 + nrm(k[21], (DEPTH, D_MODEL), 0.02),
        'w_mem_kv': nrm(k[22], (DEPTH, D_MODEL, 2 * MEM_W), D_MODEL ** -0.5),
        'g_ffn': 1.0 + nrm(k[23], (DEPTH, D_MODEL), 0.02),
        'w_pq': nrm(k[24], (DEPTH, D_MODEL, PEER_HEADS * PEER_DKEY), D_MODEL ** -0.5),
        'peer_k1': nrm(k[25], (DEPTH, PEER_NKEYS, PEER_DKEY // 2), (PEER_DKEY // 2) ** -0.5),
        'peer_k2': nrm(k[26], (DEPTH, PEER_NKEYS, PEER_DKEY // 2), (PEER_DKEY // 2) ** -0.5),
        'peer_u': nrm(k[27], (DEPTH, PEER_NEXP, D_MODEL), D_MODEL ** -0.5),
        'peer_v': nrm(k[28], (DEPTH, PEER_NEXP, D_MODEL), PEER_HEADS ** -0.5),
        'g_final': 1.0 + nrm(k[29], (D_MODEL,), 0.02),
    }


def reference(x_prompt, x_sample, cache_fox_k, cache_fox_v, cache_fox_logf, state_gla, cache_mem_k, cache_mem_v,
              page_table, mem_prompt, g_mix, w_in, w_a2, b_a2, b_fgate, b_gate, g_gla_head, w_gla_o, w_fox_o,
              w_mem_o, w_out, g_mem, w_mem_kv, g_ffn, w_pq, peer_k1, peer_k2, peer_u, peer_v, g_final):
    hp = x_prompt
    hs = x_sample
    kp_l, vp_l, lfp_l, sp_l, mkp_l, mvp_l = [], [], [], [], [], []
    ks_l, vs_l, lfs_l, ss_l = [], [], [], []
    for l in range(DEPTH):
        lw = {
            'g_mix': g_mix[l], 'w_in': w_in[l], 'w_a2': w_a2[l], 'b_a2': b_a2[l], 'b_fgate': b_fgate[l],
            'b_gate': b_gate[l], 'g_gla_head': g_gla_head[l], 'w_gla_o': w_gla_o[l], 'w_fox_o': w_fox_o[l],
            'w_mem_o': w_mem_o[l], 'w_out': w_out[l], 'g_ffn': g_ffn[l], 'w_pq': w_pq[l],
            'peer_k1': peer_k1[l], 'peer_k2': peer_k2[l], 'peer_u': peer_u[l], 'peer_v': peer_v[l],
        }
        mk_p, mv_p = mem_kv(mem_prompt, g_mem[l], w_mem_kv[l])
        s0_p = jnp.zeros((hp.shape[0], GLA_HEADS, GLA_DK, GLA_DV), hp.dtype)
        hp, kp, vp, lfp, sp = hybrid_layer(hp, mk_p, mv_p, s0_p, None, None, None, lw)
        pk = gather_pages(cache_fox_k[l], page_table)
        pv = gather_pages(cache_fox_v[l], page_table)
        plf = gather_pages(cache_fox_logf[l], page_table)
        hs, ks, vs, lfs, ss = hybrid_layer(hs, cache_mem_k[l], cache_mem_v[l], state_gla[l], pk, pv, plf, lw)
        kp_l.append(kp); vp_l.append(vp); lfp_l.append(lfp); sp_l.append(sp); mkp_l.append(mk_p); mvp_l.append(mv_p)
        ks_l.append(ks); vs_l.append(vs); lfs_l.append(lfs); ss_l.append(ss)
    y_prompt = rmsnorm(hp, g_final)
    y_sample = rmsnorm(hs, g_final)
    fox_k_prompt = jnp.stack(kp_l)
    fox_v_prompt = jnp.stack(vp_l)
    fox_logf_prompt = jnp.stack(lfp_l)
    gla_state_prompt = jnp.stack(sp_l)
    mem_k_prompt = jnp.stack(mkp_l)
    mem_v_prompt = jnp.stack(mvp_l)
    fox_k_sample = jnp.stack(ks_l)
    fox_v_sample = jnp.stack(vs_l)
    fox_logf_sample = jnp.stack(lfs_l)
    gla_state_sample = jnp.stack(ss_l)
    return (y_prompt, y_sample, fox_k_prompt, fox_v_prompt, fox_logf_prompt, gla_state_prompt, mem_k_prompt,
            mem_v_prompt, fox_k_sample, fox_v_sample, fox_logf_sample, gla_state_sample)
```

```python
import functools
import math

import jax
import jax.numpy as jnp
from jax import lax
from jax.experimental import pallas as pl
from jax.experimental.pallas import tpu as pltpu

EPS = 1e-6
GLA_HEADS = 4
GLA_DK = 128
GLA_RANK = 16
GLA_TAU = 16.0
GLA_ROWS = 64
FOX_HEADS = 8
FOX_HD = 64
MEM_HEADS = 4
MEM_HD = 128
PEER_HEADS = 8
PEER_NKEYS = 128
PEER_TOPK = 16
LANES = 128
VMEM_LIMIT = 56 * 1024 * 1024
NEG_BIG = -1e30

F32 = jnp.float32
BF16 = jnp.bfloat16
HI = lax.Precision.HIGHEST


def _cparams(*sem):
    return pltpu.CompilerParams(dimension_semantics=sem, vmem_limit_bytes=VMEM_LIMIT)


def _log_sigmoid(x):
    return jnp.minimum(x, 0.0) - jnp.log1p(jnp.exp(-jnp.abs(x)))


def _dot_nt(a, b):
    return lax.dot_general(a, b, (((1,), (1,)), ((), ())), preferred_element_type=F32)


def _dot_tn(a, b):
    return lax.dot_general(a, b, (((0,), (0,)), ((), ())), preferred_element_type=F32)


def _dot(a, b, precision=None):
    return jnp.dot(a, b, preferred_element_type=F32, precision=precision)


def _rms_proj_kernel(x_ref, g_ref, w_ref, *o_refs, splits):
    x = x_ref[...]
    y = x * lax.rsqrt(jnp.mean(x * x, axis=-1, keepdims=True) + EPS)
    xn = (y * g_ref[...]).astype(BF16)
    for (off, width), o_ref in zip(splits, o_refs):
        o_ref[...] = _dot(xn, w_ref[:, off:off + width])


def rms_proj(x, g, w_bf16, widths, tm):
    n, d = x.shape
    splits, off = [], 0
    for w in widths:
        splits.append((off, w))
        off += w
    assert off == w_bf16.shape[1] and n % tm == 0
    return pl.pallas_call(
        functools.partial(_rms_proj_kernel, splits=tuple(splits)),
        grid=(n // tm,),
        in_specs=[
            pl.BlockSpec((tm, d), lambda i: (i, 0)),
            pl.BlockSpec((1, d), lambda i: (0, 0)),
            pl.BlockSpec(w_bf16.shape, lambda i: (0, 0), pipeline_mode=pl.Buffered(1)),
        ],
        out_specs=[pl.BlockSpec((tm, w), lambda i: (i, 0)) for w in widths],
        out_shape=[jax.ShapeDtypeStruct((n, w), F32) for w in widths],
        compiler_params=_cparams("parallel"),
        name="rms_proj",
    )(x, g.reshape(1, d), w_bf16)


def _fgate_kernel(z_ref, b_ref, lf_ref, d_ref, *, t_len, blk):
    r = lax.broadcasted_iota(jnp.int32, (blk, blk), 0)
    c = lax.broadcasted_iota(jnp.int32, (blk, blk), 1)
    tril = (c <= r).astype(F32)
    carry = jnp.zeros((1, LANES), F32)
    for j in range(t_len // blk):
        lf = _log_sigmoid(z_ref[j * blk:(j + 1) * blk, :] + b_ref[...])
        lf_ref[j * blk:(j + 1) * blk, :] = lf
        cs = _dot(tril, lf, precision=HI) + carry
        d_ref[j * blk:(j + 1) * blk, :] = cs
        carry = cs[blk - 1:blk, :]


def fgate(z_small, b_pad, t_len):
    n = z_small.shape[0]
    blk = min(t_len, 256)
    return pl.pallas_call(
        functools.partial(_fgate_kernel, t_len=t_len, blk=blk),
        grid=(n // t_len,),
        in_specs=[pl.BlockSpec((t_len, LANES), lambda i: (i, 0)),
                  pl.BlockSpec((1, LANES), lambda i: (0, 0))],
        out_specs=[pl.BlockSpec((t_len, LANES), lambda i: (i, 0))] * 2,
        out_shape=[jax.ShapeDtypeStruct((n, LANES), F32)] * 2,
        compiler_params=_cparams("parallel"),
        name="fgate",
    )(z_small, b_pad)


def _gla_levels(seg):
    out, m = [], seg // 2
    while m >= 1:
        out.append(m)
        m //= 2
    return out


def _gla_intra(q, k, v, la, seg):
    rows = q.shape[0]
    r = lax.broadcasted_iota(jnp.int32, (rows, rows), 0)
    c = lax.broadcasted_iota(jnp.int32, (rows, rows), 1)
    rk = lax.broadcasted_iota(jnp.int32, (rows, LANES), 0)
    levels = _gla_levels(seg)
    sel = [((c <= r) & (c // seg == r // seg)).astype(F32)]
    for m in levels:
        bound = (r // (2 * m)) * (2 * m) + m - 1
        sel.append(((c <= bound) & (c // seg == r // seg)).astype(F32))
    sums = _dot(jnp.concatenate(sel, axis=0), la, precision=HI)
    cum = sums[:rows]
    att = jnp.zeros((rows, rows), F32)
    for li, m in enumerate(levels):
        bnd = sums[(li + 1) * rows:(li + 2) * rows]
        later = (rk % (2 * m)) >= m
        qd = jnp.where(later, q * jnp.exp(jnp.where(later, cum - bnd, 0.0)), 0.0)
        kd = jnp.where(later, 0.0, k * jnp.exp(jnp.where(later, 0.0, bnd - cum)))
        a = _dot_nt(qd.astype(BF16), kd.astype(BF16))
        att = att + jnp.where(r // (2 * m) == c // (2 * m), a, 0.0)
    o = _dot(att.astype(BF16), v.astype(BF16))
    o = o + jnp.sum(q * k, axis=-1, keepdims=True) * v
    return o, cum


def _gla_la(sm_ref_rows, wa2_ref, ba2_ref):
    z = _dot(sm_ref_rows.astype(BF16), wa2_ref[...]) + ba2_ref[...]
    return _log_sigmoid(z) * (1.0 / GLA_TAU)


def _gla_prompt_kernel(q_ref, k_ref, v_ref, sm_ref, wa2_ref, ba2_ref, s0_ref, o_ref, s_ref, st_ref, *, t_len):
    st_ref[...] = s0_ref[0, 0].T

    def body(ci, _):
        rows = pl.ds(pl.multiple_of(ci * GLA_ROWS, GLA_ROWS), GLA_ROWS)
        q = q_ref[rows, :] * (GLA_DK ** -0.5)
        k = k_ref[rows, :]
        v = v_ref[rows, :]
        la = _gla_la(sm_ref[rows, :], wa2_ref, ba2_ref)
        o, cum = _gla_intra(q, k, v, la, GLA_ROWS)
        last = cum[GLA_ROWS - 1:GLA_ROWS, :]
        st = st_ref[...]
        o = o + _dot_nt((q * jnp.exp(cum)).astype(BF16), st.astype(BF16))
        kdec = (k * jnp.exp(last - cum)).astype(BF16)
        st_ref[...] = st * jnp.exp(last) + _dot_tn(v.astype(BF16), kdec)
        o_ref[rows, :] = o
        return 0

    lax.fori_loop(0, t_len // GLA_ROWS, body, 0)
    s_ref[0, 0] = st_ref[...].T


def _gla_sample_kernel(q_ref, k_ref, v_ref, sm_ref, wa2_ref, ba2_ref, s0_ref, o_ref, s_ref, *, seg):
    q = q_ref[...] * (GLA_DK ** -0.5)
    k = k_ref[...]
    v = v_ref[...]
    la = _gla_la(sm_ref[...], wa2_ref, ba2_ref)
    o, cum = _gla_intra(q, k, v, la, seg)
    qe = (q * jnp.exp(cum)).astype(BF16)
    for b in range(GLA_ROWS // seg):
        lo, hi = b * seg, (b + 1) * seg
        last = cum[hi - 1:hi, :]
        st = s0_ref[b, 0].T
        o_ref[lo:hi, :] = o[lo:hi] + _dot_nt(qe[lo:hi], st.astype(BF16))
        kdec = (k[lo:hi] * jnp.exp(last - cum[lo:hi])).astype(BF16)
        st_new = st * jnp.exp(last) + _dot_tn(v[lo:hi].astype(BF16), kdec)
        s_ref[b, 0] = st_new.T


def gla(zg, sm, w_a2_bf16, b_a2, state, n_seq, t_len):
    n = zg.shape[0]
    h = GLA_HEADS
    if t_len % GLA_ROWS == 0:
        rows, grid = t_len, (n_seq, h)
        nb = 1
        kern = functools.partial(_gla_prompt_kernel, t_len=t_len)
        scratch = [pltpu.VMEM((GLA_DK, GLA_DK), F32)]
    else:
        assert GLA_ROWS % t_len == 0
        nb = GLA_ROWS // t_len
        rows, grid = GLA_ROWS, (n_seq // nb, h)
        kern = functools.partial(_gla_sample_kernel, seg=t_len)
        scratch = []
    col = lambda base: pl.BlockSpec((rows, GLA_DK), lambda b, hh: (b, base + hh))
    st_spec = pl.BlockSpec((nb, 1, GLA_DK, GLA_DK), lambda b, hh: (b, hh, 0, 0))
    return pl.pallas_call(
        kern,
        grid=grid,
        in_specs=[col(0), col(h), col(2 * h),
                  pl.BlockSpec((rows, LANES), lambda b, hh: (b, 0)),
                  pl.BlockSpec((LANES, GLA_DK), lambda b, hh: (0, hh)),
                  pl.BlockSpec((1, GLA_DK), lambda b, hh: (0, hh)),
                  st_spec],
        out_specs=[pl.BlockSpec((rows, GLA_DK), lambda b, hh: (b, hh)), st_spec],
        out_shape=[jax.ShapeDtypeStruct((n, h * GLA_DK), F32),
                   jax.ShapeDtypeStruct(state.shape, F32)],
        scratch_shapes=scratch,
        compiler_params=_cparams("parallel", "parallel"),
        name="gla",
    )(zg, zg, zg, sm, w_a2_bf16, b_a2.reshape(1, -1), state)


def _fox_prompt_kernel(q_ref, k_ref, v_ref, dq_ref, dk_ref, o_ref, m_ref, l_ref, acc_ref, *, tq, tk):
    hp = pl.program_id(1)
    qi = pl.program_id(2)
    ki = pl.program_id(3)
    nk = pl.num_programs(3)

    @pl.when(ki == 0)
    def _():
        m_ref[...] = jnp.full(m_ref.shape, NEG_BIG, F32)
        l_ref[...] = jnp.zeros(l_ref.shape, F32)
        acc_ref[...] = jnp.zeros(acc_ref.shape, F32)

    @pl.when(ki * tk < (qi + 1) * tq)
    def _():
        lane = lax.broadcasted_iota(jnp.int32, (tq, LANES), 1)
        row = lax.broadcasted_iota(jnp.int32, (tq, tk), 0) + qi * tq
        colp = lax.broadcasted_iota(jnp.int32, (tq, tk), 1) + ki * tk
        lane8 = lax.broadcasted_iota(jnp.int32, (tq, FOX_HEADS), 1)
        sub8 = lax.broadcasted_iota(jnp.int32, (FOX_HEADS, tk), 0)
        q2 = q_ref[...] * (FOX_HD ** -0.5)
        k2 = k_ref[...].astype(BF16)
        v2 = v_ref[...].astype(BF16)
        dq = dq_ref[...]
        dk = dk_ref[...]
        for hh in range(2):
            head = hp * 2 + hh
            qh = jnp.where(lane // FOX_HD == hh, q2, 0.0).astype(BF16)
            s = _dot_nt(qh, k2)
            dqc = jnp.sum(jnp.where(lane8 == head, dq, 0.0), axis=1, keepdims=True)
            dkr = jnp.sum(jnp.where(sub8 == head, dk, 0.0), axis=0, keepdims=True)
            s = s + (dqc - dkr)
            s = jnp.where(colp <= row, s, -jnp.inf)
            m_old = m_ref[hh]
            m_new = jnp.maximum(m_old, jnp.max(s, axis=1, keepdims=True))
            alpha = jnp.exp(m_old - m_new)
            p = jnp.exp(s - m_new)
            l_ref[hh] = alpha * l_ref[hh] + jnp.sum(p, axis=1, keepdims=True)
            acc_ref[hh] = alpha * acc_ref[hh] + _dot(p.astype(BF16), v2)
            m_ref[hh] = m_new

    @pl.when(ki == nk - 1)
    def _():
        lane = lax.broadcasted_iota(jnp.int32, (tq, LANES), 1)
        o0 = acc_ref[0] / l_ref[0]
        o1 = acc_ref[1] / l_ref[1]
        o_ref[...] = jnp.where(lane < FOX_HD, o0, o1)


def fox_prompt(fq, fk, fv, d, d_t, n_seq, t_len, tq=512, tk=512):
    n = fq.shape[0]
    nq, nk = t_len // tq, t_len // tk
    kv_map = lambda b, hp, i, j: (b * nk + jnp.minimum(j, ((i + 1) * tq - 1) // tk), hp)
    return pl.pallas_call(
        functools.partial(_fox_prompt_kernel, tq=tq, tk=tk),
        grid=(n_seq, FOX_HEADS // 2, nq, nk),
        in_specs=[
            pl.BlockSpec((tq, LANES), lambda b, hp, i, j: (b * nq + i, hp)),
            pl.BlockSpec((tk, LANES), kv_map),
            pl.BlockSpec((tk, LANES), kv_map),
            pl.BlockSpec((tq, FOX_HEADS), lambda b, hp, i, j: (b * nq + i, 0)),
            pl.BlockSpec((FOX_HEADS, tk), lambda b, hp, i, j: (0, b * nk + jnp.minimum(j, ((i + 1) * tq - 1) // tk))),
        ],
        out_specs=pl.BlockSpec((tq, LANES), lambda b, hp, i, j: (b * nq + i, hp)),
        out_shape=jax.ShapeDtypeStruct((n, FOX_HEADS * FOX_HD), F32),
        scratch_shapes=[pltpu.VMEM((2, tq, 1), F32), pltpu.VMEM((2, tq, 1), F32), pltpu.VMEM((2, tq, LANES), F32)],
        compiler_params=_cparams("parallel", "parallel", "parallel", "arbitrary"),
        name="fox_prompt",
    )(fq, fk, fv, d, d_t)


def _fox_sample_kernel(pt_ref, q_ref, kn_ref, vn_ref, lfn_ref, *rest, n_pages, page, t_new):
    kp = rest[:n_pages]
    vp = rest[n_pages:2 * n_pages]
    lp = rest[2 * n_pages:3 * n_pages]
    o_ref = rest[3 * n_pages]
    s_scr = rest[3 * n_pages + 1]
    nh, w = FOX_HEADS, FOX_HEADS * FOX_HD
    cols = nh * t_new
    row_q = lax.broadcasted_iota(jnp.int32, (cols, w), 0)
    lane_q = lax.broadcasted_iota(jnp.int32, (cols, w), 1)
    qall = jnp.concatenate([q_ref[...]] * nh, axis=0) * (FOX_HD ** -0.5)
    qall = jnp.where(row_q // t_new == lane_q // FOX_HD, qall, 0.0).astype(BF16)
    expand = (lax.broadcasted_iota(jnp.int32, (nh, cols), 1) // t_new
              == lax.broadcasted_iota(jnp.int32, (nh, cols), 0)).astype(F32)
    pr = lax.broadcasted_iota(jnp.int32, (page, page), 0)
    pc = lax.broadcasted_iota(jnp.int32, (page, page), 1)
    tril = (pc <= pr).astype(F32)
    cins, offs = [], []
    off = jnp.zeros((1, cols), F32)
    for p in range(n_pages):
        cin = _dot(tril, _dot(lp[p][...], expand, precision=HI), precision=HI)
        cins.append(cin)
        offs.append(off)
        off = off + cin[page - 1:page, :]
    total = off
    nr = lax.broadcasted_iota(jnp.int32, (t_new, t_new), 0)
    nc = lax.broadcasted_iota(jnp.int32, (t_new, t_new), 1)
    cn = _dot((nc <= nr).astype(F32), _dot(lfn_ref[...], expand, precision=HI), precision=HI)
    sub_n = lax.broadcasted_iota(jnp.int32, (t_new, cols), 0)
    qpos = lax.broadcasted_iota(jnp.int32, (t_new, cols), 1) % t_new
    dq = jnp.sum(jnp.where(sub_n == qpos, cn, 0.0), axis=0, keepdims=True)
    m = jnp.full((1, cols), NEG_BIG, F32)
    for p in range(n_pages):
        s = _dot_nt(kp[p][...].astype(BF16), qall) + ((total + dq) - (offs[p] + cins[p]))
        s_scr[p * page:(p + 1) * page, :] = s
        m = jnp.maximum(m, jnp.max(s, axis=0, keepdims=True))
    sn = _dot_nt(kn_ref[...].astype(BF16), qall) + (dq - cn)
    sn = jnp.where(sub_n <= qpos, sn, -jnp.inf)
    m = jnp.maximum(m, jnp.max(sn, axis=0, keepdims=True))
    pn = jnp.exp(sn - m)
    l = jnp.sum(pn, axis=0, keepdims=True)
    for p in range(n_pages):
        e = jnp.exp(s_scr[p * page:(p + 1) * page, :] - m)
        s_scr[p * page:(p + 1) * page, :] = e
        l = l + jnp.sum(e, axis=0, keepdims=True)
    inv = 1.0 / l
    o = _dot_tn((pn * inv).astype(BF16), vn_ref[...].astype(BF16))
    for p in range(n_pages):
        pp = (s_scr[p * page:(p + 1) * page, :] * inv).astype(BF16)
        o = o + _dot_tn(pp, vp[p][...].astype(BF16))
    out = jnp.zeros((t_new, w), F32)
    lane_o = lax.broadcasted_iota(jnp.int32, (t_new, w), 1)
    for hh in range(nh):
        out = out + jnp.where(lane_o // FOX_HD == hh, o[hh * t_new:(hh + 1) * t_new, :], 0.0)
    o_ref[...] = out


def fox_sample(fq, fk, fv, lf_new, cache_k, cache_v, cache_lf, page_table, t_new):
    n, w = fq.shape
    n_seq, n_pages = page_table.shape
    page = cache_k.shape[1]
    cols = FOX_HEADS * t_new
    tok = pl.BlockSpec((t_new, w), lambda b, pt: (b, 0))
    page_spec = lambda p, width: pl.BlockSpec((None, page, width), lambda b, pt: (pt[b, p], 0, 0))
    grid_spec = pltpu.PrefetchScalarGridSpec(
        num_scalar_prefetch=1,
        grid=(n_seq,),
        in_specs=([tok, tok, tok, pl.BlockSpec((t_new, FOX_HEADS), lambda b, pt: (b, 0))]
                  + [page_spec(p, w) for p in range(n_pages)]
                  + [page_spec(p, w) for p in range(n_pages)]
                  + [page_spec(p, FOX_HEADS) for p in range(n_pages)]),
        out_specs=tok,
        scratch_shapes=[pltpu.VMEM((n_pages * page, cols), F32)],
    )
    return pl.pallas_call(
        functools.partial(_fox_sample_kernel, n_pages=n_pages, page=page, t_new=t_new),
        grid_spec=grid_spec,
        out_shape=jax.ShapeDtypeStruct((n, w), F32),
        compiler_params=_cparams("parallel"),
        name="fox_sample",
    )(page_table, fq, fk, fv, lf_new, *([cache_k] * n_pages), *([cache_v] * n_pages), *([cache_lf] * n_pages))


def _mem_attn_kernel(q_ref, k_ref, v_ref, o_ref):
    for hh in range(MEM_HEADS):
        sl = slice(hh * MEM_HD, (hh + 1) * MEM_HD)
        s = _dot_nt(q_ref[:, sl].astype(BF16), k_ref[:, sl].astype(BF16)) * (MEM_HD ** -0.5)
        s = s - jnp.max(s, axis=1, keepdims=True)
        e = jnp.exp(s)
        p = e / jnp.sum(e, axis=1, keepdims=True)
        o_ref[:, sl] = _dot(p.astype(BF16), v_ref[:, sl].astype(BF16))


def mem_attn(mq, mk, mv, n_seq, t_len, mem_len, tq):
    n, w = mq.shape
    nq = t_len // tq
    return pl.pallas_call(
        _mem_attn_kernel,
        grid=(n_seq, nq),
        in_specs=[pl.BlockSpec((tq, w), lambda b, i: (b * nq + i, 0)),
                  pl.BlockSpec((mem_len, w), lambda b, i: (b, 0)),
                  pl.BlockSpec((mem_len, w), lambda b, i: (b, 0))],
        out_specs=pl.BlockSpec((tq, w), lambda b, i: (b * nq + i, 0)),
        out_shape=jax.ShapeDtypeStruct((n, w), F32),
        compiler_params=_cparams("parallel", "parallel"),
        name="mem_attn",
    )(mq, mk, mv)


def _merge_kernel(x_ref, og_ref, r_ref, of_ref, om_ref, gt_ref, bg_ref, gh_ref,
                  wg_ref, wf_ref, wm_ref, wo_ref, h_ref):
    d = x_ref.shape[1]
    parts = []
    for hh in range(GLA_HEADS):
        sl = slice(hh * GLA_DK, (hh + 1) * GLA_DK)
        o = og_ref[:, sl]
        y = o * lax.rsqrt(jnp.mean(o * o, axis=-1, keepdims=True) + EPS)
        r = r_ref[:, sl]
        parts.append(((y * gh_ref[...]) * (r * jax.nn.sigmoid(r))).astype(BF16))
    og = jnp.concatenate(parts, axis=1)
    branches = (_dot(og, wg_ref[...]),
                _dot(of_ref[...].astype(BF16), wf_ref[...]),
                _dot(om_ref[...].astype(BF16), wm_ref[...]))
    m = None
    for bi, br in enumerate(branches):
        gate = jax.nn.sigmoid(gt_ref[:, bi * d:(bi + 1) * d] + bg_ref[:, bi * d:(bi + 1) * d])
        m = gate * br if m is None else m + gate * br
    h_ref[...] = x_ref[...] + _dot(m.astype(BF16), wo_ref[...])


def merge(x, o_gla, zg, o_fox, o_mem, gt, b_gate, g_head, wg, wf, wm, wo, tm):
    n, d = x.shape
    wv = GLA_HEADS * GLA_DK
    row = lambda width, cb=0: pl.BlockSpec((tm, width), lambda i: (i, cb))
    full = lambda a: pl.BlockSpec(a.shape, lambda i: (0,) * a.ndim)
    bg = b_gate.reshape(1, -1)
    gh = g_head.reshape(1, -1)
    return pl.pallas_call(
        _merge_kernel,
        grid=(n // tm,),
        in_specs=[row(d), row(wv), row(wv, 3), row(o_fox.shape[1]), row(o_mem.shape[1]), row(3 * d),
                  full(bg), full(gh), full(wg), full(wf), full(wm), full(wo)],
        out_specs=row(d),
        out_shape=jax.ShapeDtypeStruct((n, d), F32),
        compiler_params=_cparams("parallel"),
        name="merge",
    )(x, o_gla, zg, o_fox, o_mem, gt, bg, gh, wg, wf, wm, wo)


def _peer_combos():
    return [(a, b) for a in range(PEER_TOPK) for b in range(PEER_TOPK) if (a + 1) * (b + 1) <= PEER_TOPK]


def _extract_top(work_ref, out_ref, n_rows, tm):
    rid = lax.broadcasted_iota(jnp.int32, (n_rows, tm), 0)

    def body(it, _):
        x = work_ref[0:n_rows, :]
        mx = jnp.max(x, axis=0, keepdims=True)
        first = jnp.min(jnp.where(x == mx, rid, n_rows), axis=0, keepdims=True)
        work_ref[0:n_rows, :] = jnp.where(rid == first, -jnp.inf, x)
        out_ref[pl.ds(it, 1), :] = mx
        return 0

    lax.fori_loop(0, PEER_TOPK, body, 0)


def _gelu_tanh(x):
    u = x * (0.7978845608028654 + 0.035677408136300125 * (x * x))
    return (0.5 * x) * (1.0 + jnp.tanh(u))


def _peer_kernel(h_ref, gf_ref, wpq_ref, k1_ref, k2_ref, u_ref, vt_ref, gfin_ref, y_ref,
                 xn_ref, s1_ref, e1_ref, s2_ref, e2_ref, thr_ref, acc_ref, work_ref, v1_ref, v2_ref, cand_ref,
                 *, tm, ipb):
    step = pl.program_id(1)
    nsteps = pl.num_programs(1)
    nk = PEER_NKEYS
    combos = _peer_combos()
    n_cand = ((len(combos) + 7) // 8) * 8

    @pl.when(step == 0)
    def _():
        h = h_ref[...]
        y = h * lax.rsqrt(jnp.mean(h * h, axis=-1, keepdims=True) + EPS)
        xn = (y * gf_ref[...]).astype(BF16)
        xn_ref[...] = xn
        acc_ref[...] = jnp.zeros(acc_ref.shape, F32)

        def per_head(hd, _):
            wq = wpq_ref[hd]
            qh = _dot(xn, wq).astype(BF16)
            s1 = _dot_nt(k1_ref[...], qh[:, :nk])
            s2 = _dot_nt(k2_ref[...], qh[:, nk:])
            s1_ref[hd] = s1
            s2_ref[hd] = s2
            work_ref[0:nk, :] = s1
            _extract_top(work_ref, v1_ref, nk, tm)
            work_ref[0:nk, :] = s2
            _extract_top(work_ref, v2_ref, nk, tm)
            for ci, (a, b) in enumerate(combos):
                c = v1_ref[a:a + 1, :] + v2_ref[b:b + 1, :]
                cand_ref[ci:ci + 1, :] = c
                work_ref[ci:ci + 1, :] = c
            if n_cand > len(combos):
                pad = jnp.full((n_cand - len(combos), tm), -jnp.inf, F32)
                cand_ref[len(combos):n_cand, :] = pad
                work_ref[len(combos):n_cand, :] = pad
            _extract_top(work_ref, v1_ref, n_cand, tm)
            thr = v1_ref[PEER_TOPK - 1:PEER_TOPK, :]
            top = v1_ref[0:1, :]
            cand = cand_ref[...]
            z = jnp.sum(jnp.where(cand >= thr, jnp.exp(cand - top), 0.0), axis=0, keepdims=True)
            m1 = jnp.max(s1, axis=0, keepdims=True)
            m2 = jnp.max(s2, axis=0, keepdims=True)
            e1_ref[hd] = jnp.exp(s1 - m1)
            e2_ref[hd] = jnp.exp(s2 - m2) / z
            thr_ref[pl.ds(hd, 1), :] = thr
            return 0

        lax.fori_loop(0, PEER_HEADS, per_head, 0)

    a_t = _gelu_tanh(_dot_nt(u_ref[...], xn_ref[...]))
    was = []
    for ii in range(ipb):
        i = step * ipb + ii
        w = jnp.zeros((nk, tm), F32)
        for hd in range(PEER_HEADS):
            s1row = s1_ref[hd, pl.ds(i, 1), :]
            e1row = e1_ref[hd, pl.ds(i, 1), :]
            keep = (s1row + s2_ref[hd]) >= thr_ref[hd:hd + 1, :]
            w = w + jnp.where(keep, e2_ref[hd], 0.0) * e1row
        was.append((w * a_t[ii * nk:(ii + 1) * nk, :]).astype(BF16))
    wa = jnp.concatenate(was, axis=0) if ipb > 1 else was[0]
    acc_ref[...] += _dot(vt_ref[...], wa)

    @pl.when(step == nsteps - 1)
    def _():
        hh = h_ref[...] + acc_ref[...].T
        y = hh * lax.rsqrt(jnp.mean(hh * hh, axis=-1, keepdims=True) + EPS)
        y_ref[...] = y * gfin_ref[...]


def peer(h, g_ffn, wpq_heads, k1_bf16, k2_bf16, u_bf16, vt_bf16, g_final, tm, ipb=2):
    n, d = h.shape
    nk = PEER_NKEYS
    n_cand = ((len(_peer_combos()) + 7) // 8) * 8
    full = lambda a: pl.BlockSpec(a.shape, lambda t, s: (0,) * a.ndim)
    gf = g_ffn.reshape(1, d)
    gfin = g_final.reshape(1, d)
    return pl.pallas_call(
        functools.partial(_peer_kernel, tm=tm, ipb=ipb),
        grid=(n // tm, nk // ipb),
        in_specs=[pl.BlockSpec((tm, d), lambda t, s: (t, 0)), full(gf), full(wpq_heads), full(k1_bf16), full(k2_bf16),
                  pl.BlockSpec((ipb * nk, d), lambda t, s: (s, 0)),
                  pl.BlockSpec((d, ipb * nk), lambda t, s: (0, s)),
                  full(gfin)],
        out_specs=pl.BlockSpec((tm, d), lambda t, s: (t, 0)),
        out_shape=jax.ShapeDtypeStruct((n, d), F32),
        scratch_shapes=[
            pltpu.VMEM((tm, d), BF16),
            pltpu.VMEM((PEER_HEADS, nk, tm), F32), pltpu.VMEM((PEER_HEADS, nk, tm), F32),
            pltpu.VMEM((PEER_HEADS, nk, tm), F32), pltpu.VMEM((PEER_HEADS, nk, tm), F32),
            pltpu.VMEM((PEER_HEADS, tm), F32),
            pltpu.VMEM((d, tm), F32),
            pltpu.VMEM((nk, tm), F32),
            pltpu.VMEM((PEER_TOPK, tm), F32), pltpu.VMEM((PEER_TOPK, tm), F32),
            pltpu.VMEM((n_cand, tm), F32),
        ],
        compiler_params=_cparams("parallel", "arbitrary"),
        name="peer",
    )(h, gf, wpq_heads, k1_bf16, k2_bf16, u_bf16, vt_bf16, gfin)


IN_GROUP_WIDTHS = (4 * 512, 512, 512, 512, 512, 3 * 1024, LANES)


def _prep_w_in(w_in):
    gq, gk, gv, gr = w_in[:, 0:512], w_in[:, 512:1024], w_in[:, 1024:1536], w_in[:, 1536:2048]
    glr = w_in[:, 2048:2064]
    fq, fk, fv = w_in[:, 2064:2576], w_in[:, 2576:3088], w_in[:, 3088:3600]
    ff = w_in[:, 3600:3608]
    mq = w_in[:, 3608:4120]
    gt = w_in[:, 4120:]
    small = jnp.concatenate([glr, ff, jnp.zeros((w_in.shape[0], LANES - GLA_RANK - FOX_HEADS), w_in.dtype)], axis=1)
    return jnp.concatenate([gq, gk, gv, gr, fq, fk, fv, mq, gt, small], axis=1).astype(BF16)


def _layer(x3, w, state, mk, mv, past):
    n_seq, t_len, d = x3.shape
    n = n_seq * t_len
    x = x3.reshape(n, d)
    tm = min(256, n)
    zg, fq, fk, fv, mq, gt, sm = rms_proj(x, w["g_mix"], w["w_in"], IN_GROUP_WIDTHS, tm)
    lf_full, d_full = fgate(sm, w["b_fgate_pad"], t_len)
    logf = lf_full[:, GLA_RANK:GLA_RANK + FOX_HEADS]
    o_gla, s_new = gla(zg, sm, w["w_a2"], w["b_a2"], state, n_seq, t_len)
    if past is None:
        dd = d_full[:, GLA_RANK:GLA_RANK + FOX_HEADS]
        tq = min(512, t_len)
        o_fox = fox_prompt(fq, fk, fv, dd, dd.T, n_seq, t_len, tq=tq, tk=tq)
        o_mem = mem_attn(mq, mk, mv, n_seq, t_len, mk.shape[0] // n_seq, tq=tq)
    else:
        cache_k, cache_v, cache_lf, page_table = past
        o_fox = fox_sample(fq, fk, fv, logf, cache_k, cache_v, cache_lf, page_table, t_len)
        o_mem = mem_attn(mq, mk, mv, n_seq, t_len, mk.shape[0] // n_seq, tq=t_len)
    h = merge(x, o_gla, zg, o_fox, o_mem, gt, w["b_gate"], w["g_gla_head"],
              w["w_gla_o"], w["w_fox_o"], w["w_mem_o"], w["w_out"], tm=min(512, n))
    y = peer(h, w["g_ffn"], w["w_pq"], w["peer_k1"], w["peer_k2"], w["peer_u"], w["peer_vt"], w["g_final"],
             tm=min(512, n))
    return y.reshape(n_seq, t_len, d), fk, fv, logf, s_new


def kernel(x_prompt, x_sample, cache_fox_k, cache_fox_v, cache_fox_logf, state_gla, cache_mem_k, cache_mem_v, page_table, mem_prompt, g_mix, w_in, w_a2, b_a2, b_fgate, b_gate, g_gla_head, w_gla_o, w_fox_o, w_mem_o, w_out, g_mem, w_mem_kv, g_ffn, w_pq, peer_k1, peer_k2, peer_u, peer_v, g_final):
    depth = w_in.shape[0]
    assert depth == 1
    l = 0
    bp, tp, d = x_prompt.shape
    bs, ts, _ = x_sample.shape
    nk = PEER_NKEYS
    w = {
        "g_mix": g_mix[l], "w_in": _prep_w_in(w_in[l]), "w_a2": jnp.zeros((LANES, w_a2.shape[2]), BF16).at[:GLA_RANK].set(w_a2[l].astype(BF16)), "b_a2": b_a2[l],
        "b_fgate_pad": jnp.zeros((1, LANES), F32).at[0, GLA_RANK:GLA_RANK + FOX_HEADS].set(b_fgate[l]),
        "b_gate": b_gate[l], "g_gla_head": g_gla_head[l],
        "w_gla_o": w_gla_o[l].astype(BF16), "w_fox_o": w_fox_o[l].astype(BF16),
        "w_mem_o": w_mem_o[l].astype(BF16), "w_out": w_out[l].astype(BF16),
        "g_ffn": g_ffn[l],
        "w_pq": w_pq[l].reshape(d, PEER_HEADS, 2 * nk).transpose(1, 0, 2).astype(BF16),
        "peer_k1": peer_k1[l].astype(BF16), "peer_k2": peer_k2[l].astype(BF16),
        "peer_u": peer_u[l].astype(BF16), "peer_vt": peer_v[l].astype(BF16).T,
        "g_final": g_final,
    }
    mem_len = mem_prompt.shape[1]
    mw = MEM_HEADS * MEM_HD
    mk_p, mv_p = rms_proj(mem_prompt.reshape(bp * mem_len, d), g_mem[l], w_mem_kv[l].astype(BF16), (mw, mw),
                          tm=256)
    s0_p = jnp.zeros((bp, GLA_HEADS, GLA_DK, GLA_DK), F32)
    y_p, kp, vp, lfp, sp = _layer(x_prompt, w, s0_p, mk_p, mv_p, None)
    n_phys, page = cache_fox_k.shape[1], cache_fox_k.shape[2]
    fw = FOX_HEADS * FOX_HD
    past = (cache_fox_k[l].reshape(n_phys, page, fw), cache_fox_v[l].reshape(n_phys, page, fw),
            cache_fox_logf[l], page_table)
    y_s, ks, vs, lfs, ss = _layer(x_sample, w, state_gla[l], cache_mem_k[l].reshape(bs * mem_len, mw),
                                  cache_mem_v[l].reshape(bs * mem_len, mw), past)
    return (y_p, y_s,
            kp.reshape(1, bp, tp, FOX_HEADS, FOX_HD), vp.reshape(1, bp, tp, FOX_HEADS, FOX_HD),
            lfp.reshape(1, bp, tp, FOX_HEADS), sp[None],
            mk_p.reshape(1, bp, mem_len, MEM_HEADS, MEM_HD), mv_p.reshape(1, bp, mem_len, MEM_HEADS, MEM_HD),
            ks.reshape(1, bs, ts, FOX_HEADS, FOX_HD), vs.reshape(1, bs, ts, FOX_HEADS, FOX_HD),
            lfs.reshape(1, bs, ts, FOX_HEADS), ss[None])
```

```python
import functools

import jax
import jax.numpy as jnp
from jax import lax
from jax.experimental import pallas as pl
from jax.experimental.pallas import tpu as pltpu

EPS = 1e-6
GLA_HEADS = 4
GLA_DK = 128
GLA_RANK = 16
GLA_TAU = 16.0
GLA_ROWS = 64
FOX_HEADS = 8
FOX_HD = 64
MEM_HEADS = 4
MEM_HD = 128
PEER_HEADS = 8
PEER_NKEYS = 128
PEER_TOPK = 16
LANES = 128
SUBLANES = 8
PACKED_ROWS = 16
VMEM_LIMIT = 56 * 1024 * 1024
NEG_BIG = -1e30

F32 = jnp.float32
BF16 = jnp.bfloat16
HI = lax.Precision.HIGHEST


def _cparams(*sem):
    return pltpu.CompilerParams(dimension_semantics=sem, vmem_limit_bytes=VMEM_LIMIT)


def _log_sigmoid(x):
    return jnp.minimum(x, 0.0) - jnp.log1p(jnp.exp(-jnp.abs(x)))


def _dot_nt(a, b, precision=None):
    return lax.dot_general(a, b, (((1,), (1,)), ((), ())), preferred_element_type=F32, precision=precision)


def _dot_tn(a, b):
    return lax.dot_general(a, b, (((0,), (0,)), ((), ())), preferred_element_type=F32)


def _dot(a, b, precision=None):
    return jnp.dot(a, b, preferred_element_type=F32, precision=precision)


def _rms_proj_kernel(x_ref, g_ref, w_ref, *o_refs, splits):
    x = x_ref[...]
    y = x * lax.rsqrt(jnp.mean(x * x, axis=-1, keepdims=True) + EPS)
    xn = (y * g_ref[...]).astype(BF16)
    for (off, width), o_ref in zip(splits, o_refs):
        o_ref[...] = _dot(xn, w_ref[:, off:off + width])


def rms_proj(x, g, w_bf16, widths, tm):
    n, d = x.shape
    splits, off = [], 0
    for w in widths:
        splits.append((off, w))
        off += w
    assert off == w_bf16.shape[1] and n % tm == 0
    return pl.pallas_call(
        functools.partial(_rms_proj_kernel, splits=tuple(splits)),
        grid=(n // tm,),
        in_specs=[
            pl.BlockSpec((tm, d), lambda i: (i, 0)),
            pl.BlockSpec((1, d), lambda i: (0, 0)),
            pl.BlockSpec(w_bf16.shape, lambda i: (0, 0), pipeline_mode=pl.Buffered(1)),
        ],
        out_specs=[pl.BlockSpec((tm, w), lambda i: (i, 0)) for w in widths],
        out_shape=[jax.ShapeDtypeStruct((n, w), F32) for w in widths],
        compiler_params=_cparams("parallel"),
        name="rms_proj",
    )(x, g.reshape(1, d), w_bf16)


def _fgate_kernel(z_ref, b_ref, lf_ref, d_ref, *, t_len, blk):
    r = lax.broadcasted_iota(jnp.int32, (blk, blk), 0)
    c = lax.broadcasted_iota(jnp.int32, (blk, blk), 1)
    tril = (c <= r).astype(F32)
    carry = jnp.zeros((1, LANES), F32)
    for j in range(t_len // blk):
        lf = _log_sigmoid(z_ref[j * blk:(j + 1) * blk, :] + b_ref[...])
        lf_ref[j * blk:(j + 1) * blk, :] = lf
        cs = _dot(tril, lf, precision=HI) + carry
        d_ref[j * blk:(j + 1) * blk, :] = cs
        carry = cs[blk - 1:blk, :]


def fgate(z_small, b_pad, t_len):
    n = z_small.shape[0]
    blk = min(t_len, 256)
    return pl.pallas_call(
        functools.partial(_fgate_kernel, t_len=t_len, blk=blk),
        grid=(n // t_len,),
        in_specs=[pl.BlockSpec((t_len, LANES), lambda i: (i, 0)),
                  pl.BlockSpec((1, LANES), lambda i: (0, 0))],
        out_specs=[pl.BlockSpec((t_len, LANES), lambda i: (i, 0))] * 2,
        out_shape=[jax.ShapeDtypeStruct((n, LANES), F32)] * 2,
        compiler_params=_cparams("parallel"),
        name="fgate",
    )(z_small, b_pad)


def _gla_levels(seg):
    out, m = [], seg // 2
    while m >= 1:
        out.append(m)
        m //= 2
    return out


def _gla_intra(q, k, v, la, seg):
    rows = q.shape[0]
    r = lax.broadcasted_iota(jnp.int32, (rows, rows), 0)
    c = lax.broadcasted_iota(jnp.int32, (rows, rows), 1)
    rk = lax.broadcasted_iota(jnp.int32, (rows, LANES), 0)
    levels = _gla_levels(seg)
    sel = [((c <= r) & (c // seg == r // seg)).astype(F32)]
    for m in levels:
        bound = (r // (2 * m)) * (2 * m) + m - 1
        sel.append(((c <= bound) & (c // seg == r // seg)).astype(F32))
    sums = _dot(jnp.concatenate(sel, axis=0), la, precision=HI)
    cum = sums[:rows]
    att = jnp.zeros((rows, rows), F32)
    for li, m in enumerate(levels):
        bnd = sums[(li + 1) * rows:(li + 2) * rows]
        later = (rk % (2 * m)) >= m
        qd = jnp.where(later, q * jnp.exp(jnp.where(later, cum - bnd, 0.0)), 0.0)
        kd = jnp.where(later, 0.0, k * jnp.exp(jnp.where(later, 0.0, bnd - cum)))
        a = _dot_nt(qd.astype(BF16), kd.astype(BF16))
        att = att + jnp.where(r // (2 * m) == c // (2 * m), a, 0.0)
    o = _dot(att.astype(BF16), v.astype(BF16))
    o = o + jnp.sum(q * k, axis=-1, keepdims=True) * v
    return o, cum


def _gla_la(sm_ref_rows, wa2_ref, ba2_ref):
    z = _dot(sm_ref_rows.astype(BF16), wa2_ref[...]) + ba2_ref[...]
    return _log_sigmoid(z) * (1.0 / GLA_TAU)


def _gla_prompt_kernel(q_ref, k_ref, v_ref, sm_ref, wa2_ref, ba2_ref, s0_ref, o_ref, s_ref, st_ref, *, t_len):
    st_ref[...] = s0_ref[0, 0].T

    def body(ci, _):
        rows = pl.ds(pl.multiple_of(ci * GLA_ROWS, GLA_ROWS), GLA_ROWS)
        q = q_ref[rows, :] * (GLA_DK ** -0.5)
        k = k_ref[rows, :]
        v = v_ref[rows, :]
        la = _gla_la(sm_ref[rows, :], wa2_ref, ba2_ref)
        o, cum = _gla_intra(q, k, v, la, GLA_ROWS)
        last = cum[GLA_ROWS - 1:GLA_ROWS, :]
        st = st_ref[...]
        o = o + _dot_nt((q * jnp.exp(cum)).astype(BF16), st.astype(BF16))
        kdec = (k * jnp.exp(last - cum)).astype(BF16)
        st_ref[...] = st * jnp.exp(last) + _dot_tn(v.astype(BF16), kdec)
        o_ref[rows, :] = o
        return 0

    lax.fori_loop(0, t_len // GLA_ROWS, body, 0)
    s_ref[0, 0] = st_ref[...].T


def _gla_sample_kernel(q_ref, k_ref, v_ref, sm_ref, wa2_ref, ba2_ref, s0_ref, o_ref, s_ref, *, seg):
    q = q_ref[...] * (GLA_DK ** -0.5)
    k = k_ref[...]
    v = v_ref[...]
    la = _gla_la(sm_ref[...], wa2_ref, ba2_ref)
    o, cum = _gla_intra(q, k, v, la, seg)
    qe = (q * jnp.exp(cum)).astype(BF16)
    for b in range(GLA_ROWS // seg):
        lo, hi = b * seg, (b + 1) * seg
        last = cum[hi - 1:hi, :]
        st = s0_ref[b, 0].T
        o_ref[lo:hi, :] = o[lo:hi] + _dot_nt(qe[lo:hi], st.astype(BF16))
        kdec = (k[lo:hi] * jnp.exp(last - cum[lo:hi])).astype(BF16)
        st_new = st * jnp.exp(last) + _dot_tn(v[lo:hi].astype(BF16), kdec)
        s_ref[b, 0] = st_new.T


def gla(zg, sm, w_a2_pad, b_a2, state, n_seq, t_len):
    n = zg.shape[0]
    h = GLA_HEADS
    if t_len % GLA_ROWS == 0:
        rows, grid = t_len, (n_seq, h)
        nb = 1
        kern = functools.partial(_gla_prompt_kernel, t_len=t_len)
        scratch = [pltpu.VMEM((GLA_DK, GLA_DK), F32)]
    else:
        assert GLA_ROWS % t_len == 0
        nb = GLA_ROWS // t_len
        rows, grid = GLA_ROWS, (n_seq // nb, h)
        kern = functools.partial(_gla_sample_kernel, seg=t_len)
        scratch = []
    col = lambda base: pl.BlockSpec((rows, GLA_DK), lambda b, hh: (b, base + hh))
    st_spec = pl.BlockSpec((nb, 1, GLA_DK, GLA_DK), lambda b, hh: (b, hh, 0, 0))
    return pl.pallas_call(
        kern,
        grid=grid,
        in_specs=[col(0), col(h), col(2 * h),
                  pl.BlockSpec((rows, LANES), lambda b, hh: (b, 0)),
                  pl.BlockSpec((LANES, GLA_DK), lambda b, hh: (0, hh)),
                  pl.BlockSpec((1, GLA_DK), lambda b, hh: (0, hh)),
                  st_spec],
        out_specs=[pl.BlockSpec((rows, GLA_DK), lambda b, hh: (b, hh)), st_spec],
        out_shape=[jax.ShapeDtypeStruct((n, h * GLA_DK), F32),
                   jax.ShapeDtypeStruct(state.shape, F32)],
        scratch_shapes=scratch,
        compiler_params=_cparams("parallel", "parallel"),
        name="gla",
    )(zg, zg, zg, sm, w_a2_pad, b_a2.reshape(1, -1), state)


def _fox_prompt_kernel(q_ref, k_ref, v_ref, dq_ref, dk_ref, o_ref, m_ref, l_ref, acc_ref, *, tq, tk):
    hp = pl.program_id(1)
    qi = pl.program_id(2)
    ki = pl.program_id(3)
    nk = pl.num_programs(3)

    @pl.when(ki == 0)
    def _():
        m_ref[...] = jnp.full(m_ref.shape, NEG_BIG, F32)
        l_ref[...] = jnp.zeros(l_ref.shape, F32)
        acc_ref[...] = jnp.zeros(acc_ref.shape, F32)

    @pl.when(ki * tk < (qi + 1) * tq)
    def _():
        lane = lax.broadcasted_iota(jnp.int32, (tq, LANES), 1)
        row = lax.broadcasted_iota(jnp.int32, (tq, tk), 0) + qi * tq
        colp = lax.broadcasted_iota(jnp.int32, (tq, tk), 1) + ki * tk
        lane8 = lax.broadcasted_iota(jnp.int32, (tq, FOX_HEADS), 1)
        sub8 = lax.broadcasted_iota(jnp.int32, (FOX_HEADS, tk), 0)
        q2 = q_ref[...] * (FOX_HD ** -0.5)
        k2 = k_ref[...].astype(BF16)
        v2 = v_ref[...].astype(BF16)
        dq = dq_ref[...]
        dk = dk_ref[...]
        for hh in range(2):
            head = hp * 2 + hh
            qh = jnp.where(lane // FOX_HD == hh, q2, 0.0).astype(BF16)
            s = _dot_nt(qh, k2)
            dqc = jnp.sum(jnp.where(lane8 == head, dq, 0.0), axis=1, keepdims=True)
            dkr = jnp.sum(jnp.where(sub8 == head, dk, 0.0), axis=0, keepdims=True)
            s = s + (dqc - dkr)
            s = jnp.where(colp <= row, s, -jnp.inf)
            m_old = m_ref[hh]
            m_new = jnp.maximum(m_old, jnp.max(s, axis=1, keepdims=True))
            alpha = jnp.exp(m_old - m_new)
            p = jnp.exp(s - m_new)
            l_ref[hh] = alpha * l_ref[hh] + jnp.sum(p, axis=1, keepdims=True)
            acc_ref[hh] = alpha * acc_ref[hh] + _dot(p.astype(BF16), v2)
            m_ref[hh] = m_new

    @pl.when(ki == nk - 1)
    def _():
        lane = lax.broadcasted_iota(jnp.int32, (tq, LANES), 1)
        o0 = acc_ref[0] / l_ref[0]
        o1 = acc_ref[1] / l_ref[1]
        o_ref[...] = jnp.where(lane < FOX_HD, o0, o1)


def fox_prompt(fq, fk, fv, d, d_t, n_seq, t_len, tq=512, tk=512):
    n = fq.shape[0]
    nq, nk = t_len // tq, t_len // tk
    kv_map = lambda b, hp, i, j: (b * nk + jnp.minimum(j, ((i + 1) * tq - 1) // tk), hp)
    return pl.pallas_call(
        functools.partial(_fox_prompt_kernel, tq=tq, tk=tk),
        grid=(n_seq, FOX_HEADS // 2, nq, nk),
        in_specs=[
            pl.BlockSpec((tq, LANES), lambda b, hp, i, j: (b * nq + i, hp)),
            pl.BlockSpec((tk, LANES), kv_map),
            pl.BlockSpec((tk, LANES), kv_map),
            pl.BlockSpec((tq, FOX_HEADS), lambda b, hp, i, j: (b * nq + i, 0)),
            pl.BlockSpec((FOX_HEADS, tk), lambda b, hp, i, j: (0, b * nk + jnp.minimum(j, ((i + 1) * tq - 1) // tk))),
        ],
        out_specs=pl.BlockSpec((tq, LANES), lambda b, hp, i, j: (b * nq + i, hp)),
        out_shape=jax.ShapeDtypeStruct((n, FOX_HEADS * FOX_HD), F32),
        scratch_shapes=[pltpu.VMEM((2, tq, 1), F32), pltpu.VMEM((2, tq, 1), F32), pltpu.VMEM((2, tq, LANES), F32)],
        compiler_params=_cparams("parallel", "parallel", "parallel", "arbitrary"),
        name="fox_prompt",
    )(fq, fk, fv, d, d_t)


def _rep_rows(x, times):
    return jnp.concatenate([jnp.broadcast_to(x[i:i + 1, :], (times, x.shape[1])) for i in range(x.shape[0])], axis=0)


def _fox_sample_kernel(pt_ref, q_ref, kn_ref, vn_ref, lfn_ref, *rest, n_pages, page, t_new):
    kp = rest[:n_pages]
    vp = rest[n_pages:2 * n_pages]
    lp = rest[2 * n_pages:3 * n_pages]
    o_ref = rest[3 * n_pages]
    s_scr = rest[3 * n_pages + 1]
    nh, w = FOX_HEADS, FOX_HEADS * FOX_HD
    rows = nh * t_new
    row_q = lax.broadcasted_iota(jnp.int32, (rows, w), 0)
    lane_q = lax.broadcasted_iota(jnp.int32, (rows, w), 1)
    qall = jnp.concatenate([q_ref[...]] * nh, axis=0) * (FOX_HD ** -0.5)
    qall = jnp.where(row_q // t_new == lane_q // FOX_HD, qall, 0.0).astype(BF16)
    kr = lax.broadcasted_iota(jnp.int32, (page, page), 0)
    kc = lax.broadcasted_iota(jnp.int32, (page, page), 1)
    triu = (kr <= kc).astype(F32)
    cins, offs = [], []
    off = jnp.zeros((nh, 1), F32)
    for p in range(n_pages):
        cin = _dot(lp[p][...], triu, precision=HI)
        cins.append(cin)
        offs.append(off)
        off = off + cin[:, page - 1:page]
    total = off
    nr = lax.broadcasted_iota(jnp.int32, (t_new, t_new), 0)
    nc = lax.broadcasted_iota(jnp.int32, (t_new, t_new), 1)
    cn = _dot((nc <= nr).astype(F32), lfn_ref[...], precision=HI)
    cn_t = _dot_nt((nr == nc).astype(F32), cn, precision=HI)
    lane_h = lax.broadcasted_iota(jnp.int32, (rows, nh), 1)
    row_h = lax.broadcasted_iota(jnp.int32, (rows, nh), 0) // t_new
    cn_rows = jnp.concatenate([cn] * nh, axis=0)
    dq_new = jnp.sum(jnp.where(lane_h == row_h, cn_rows, 0.0), axis=1, keepdims=True)
    dq = dq_new + _rep_rows(total, t_new)
    m = jnp.full((rows, 1), NEG_BIG, F32)
    for p in range(n_pages):
        kt = kp[p][...].reshape(w, page).astype(BF16)
        s = _dot(qall, kt) + (dq - _rep_rows(offs[p] + cins[p], t_new))
        s_scr[p] = s
        m = jnp.maximum(m, jnp.max(s, axis=1, keepdims=True))
    sn = _dot_nt(qall, kn_ref[...].astype(BF16)) + (dq_new - _rep_rows(cn_t, t_new))
    key_n = lax.broadcasted_iota(jnp.int32, (rows, t_new), 1)
    qpos = lax.broadcasted_iota(jnp.int32, (rows, t_new), 0) % t_new
    sn = jnp.where(key_n <= qpos, sn, -jnp.inf)
    m = jnp.maximum(m, jnp.max(sn, axis=1, keepdims=True))
    pn = jnp.exp(sn - m)
    l = jnp.sum(pn, axis=1, keepdims=True)
    for p in range(n_pages):
        e = jnp.exp(s_scr[p] - m)
        s_scr[p] = e
        l = l + jnp.sum(e, axis=1, keepdims=True)
    inv = 1.0 / l
    o = _dot((pn * inv).astype(BF16), vn_ref[...].astype(BF16))
    for p in range(n_pages):
        vt = vp[p][...].reshape(w, page).astype(BF16)
        o = o + _dot_nt((s_scr[p] * inv).astype(BF16), vt)
    out = jnp.zeros((t_new, w), F32)
    lane_o = lax.broadcasted_iota(jnp.int32, (t_new, w), 1)
    for hh in range(nh):
        out = out + jnp.where(lane_o // FOX_HD == hh, o[hh * t_new:(hh + 1) * t_new, :], 0.0)
    o_ref[...] = out


def fox_sample(fq, fk, fv, lf_new, cache_kt, cache_vt, cache_lft, page_table, t_new):
    n, w = fq.shape
    n_seq, n_pages = page_table.shape
    page = cache_kt.shape[-1]
    rows = FOX_HEADS * t_new
    tok = pl.BlockSpec((t_new, w), lambda b, pt: (b, 0))
    kv_spec = lambda p: pl.BlockSpec((None, FOX_HEADS, FOX_HD, page), lambda b, pt: (pt[b, p], 0, 0, 0))
    lf_spec = lambda p: pl.BlockSpec((None, FOX_HEADS, page), lambda b, pt: (pt[b, p], 0, 0))
    grid_spec = pltpu.PrefetchScalarGridSpec(
        num_scalar_prefetch=1,
        grid=(n_seq,),
        in_specs=([tok, tok, tok, pl.BlockSpec((t_new, FOX_HEADS), lambda b, pt: (b, 0))]
                  + [kv_spec(p) for p in range(n_pages)]
                  + [kv_spec(p) for p in range(n_pages)]
                  + [lf_spec(p) for p in range(n_pages)]),
        out_specs=tok,
        scratch_shapes=[pltpu.VMEM((n_pages, rows, page), F32)],
    )
    return pl.pallas_call(
        functools.partial(_fox_sample_kernel, n_pages=n_pages, page=page, t_new=t_new),
        grid_spec=grid_spec,
        out_shape=jax.ShapeDtypeStruct((n, w), F32),
        compiler_params=_cparams("parallel"),
        name="fox_sample",
    )(page_table, fq, fk, fv, lf_new, *([cache_kt] * n_pages), *([cache_vt] * n_pages), *([cache_lft] * n_pages))


def _mem_attn_kernel(q_ref, k_ref, v_ref, o_ref):
    for hh in range(MEM_HEADS):
        sl = slice(hh * MEM_HD, (hh + 1) * MEM_HD)
        s = _dot_nt(q_ref[:, sl].astype(BF16), k_ref[:, sl].astype(BF16)) * (MEM_HD ** -0.5)
        s = s - jnp.max(s, axis=1, keepdims=True)
        e = jnp.exp(s)
        p = e / jnp.sum(e, axis=1, keepdims=True)
        o_ref[:, sl] = _dot(p.astype(BF16), v_ref[:, sl].astype(BF16))


def mem_attn(mq, mk, mv, n_seq, t_len, mem_len, tq):
    n, w = mq.shape
    nq = t_len // tq
    return pl.pallas_call(
        _mem_attn_kernel,
        grid=(n_seq, nq),
        in_specs=[pl.BlockSpec((tq, w), lambda b, i: (b * nq + i, 0)),
                  pl.BlockSpec((mem_len, w), lambda b, i: (b, 0)),
                  pl.BlockSpec((mem_len, w), lambda b, i: (b, 0))],
        out_specs=pl.BlockSpec((tq, w), lambda b, i: (b * nq + i, 0)),
        out_shape=jax.ShapeDtypeStruct((n, w), F32),
        compiler_params=_cparams("parallel", "parallel"),
        name="mem_attn",
    )(mq, mk, mv)


def _merge_kernel(x_ref, og_ref, r_ref, of_ref, om_ref, gt_ref, bg_ref, gh_ref,
                  wg_ref, wf_ref, wm_ref, wo_ref, h_ref):
    d = x_ref.shape[1]
    parts = []
    for hh in range(GLA_HEADS):
        sl = slice(hh * GLA_DK, (hh + 1) * GLA_DK)
        o = og_ref[:, sl]
        y = o * lax.rsqrt(jnp.mean(o * o, axis=-1, keepdims=True) + EPS)
        r = r_ref[:, sl]
        parts.append(((y * gh_ref[...]) * (r * jax.nn.sigmoid(r))).astype(BF16))
    og = jnp.concatenate(parts, axis=1)
    branches = (_dot(og, wg_ref[...]),
                _dot(of_ref[...].astype(BF16), wf_ref[...]),
                _dot(om_ref[...].astype(BF16), wm_ref[...]))
    m = None
    for bi, br in enumerate(branches):
        gate = jax.nn.sigmoid(gt_ref[:, bi * d:(bi + 1) * d] + bg_ref[:, bi * d:(bi + 1) * d])
        m = gate * br if m is None else m + gate * br
    h_ref[...] = x_ref[...] + _dot(m.astype(BF16), wo_ref[...])


def merge(x, o_gla, zg, o_fox, o_mem, gt, b_gate, g_head, wg, wf, wm, wo, tm):
    n, d = x.shape
    wv = GLA_HEADS * GLA_DK
    row = lambda width, cb=0: pl.BlockSpec((tm, width), lambda i: (i, cb))
    full = lambda a: pl.BlockSpec(a.shape, lambda i: (0,) * a.ndim)
    bg = b_gate.reshape(1, -1)
    gh = g_head.reshape(1, -1)
    return pl.pallas_call(
        _merge_kernel,
        grid=(n // tm,),
        in_specs=[row(d), row(wv), row(wv, 3), row(o_fox.shape[1]), row(o_mem.shape[1]), row(3 * d),
                  full(bg), full(gh), full(wg), full(wf), full(wm), full(wo)],
        out_specs=row(d),
        out_shape=jax.ShapeDtypeStruct((n, d), F32),
        compiler_params=_cparams("parallel"),
        name="merge",
    )(x, o_gla, zg, o_fox, o_mem, gt, bg, gh, wg, wf, wm, wo)


def _sort_network(n):
    def merge(lo, hi, r):
        step = r * 2
        if step < hi - lo:
            yield from merge(lo, hi, step)
            yield from merge(lo + r, hi, step)
            yield from [(i, i + r) for i in range(lo + r, hi - r, step)]
        else:
            yield (lo, lo + r)

    def sort(lo, hi):
        if hi - lo >= 1:
            mid = lo + (hi - lo) // 2
            yield from sort(lo, mid)
            yield from sort(mid + 1, hi)
            yield from merge(lo, hi, 1)

    return list(sort(0, n - 1))


def _top_sorted(slabs):
    x = list(slabs)
    k = len(x)
    for i, j in _sort_network(k):
        x[i], x[j] = jnp.maximum(x[i], x[j]), jnp.minimum(x[i], x[j])
    shift = SUBLANES // 2
    while shift >= 1:
        other = [pltpu.roll(v, shift, 0) for v in x]
        x = [jnp.maximum(x[r], other[k - 1 - r]) for r in range(k)]
        d = k // 2
        while d >= 1:
            for i in range(k):
                if i & d == 0:
                    x[i], x[i + d] = jnp.maximum(x[i], x[i + d]), jnp.minimum(x[i], x[i + d])
            d //= 2
        shift //= 2
    return x


def _along_sublanes(vals, sub):
    out = vals[-1]
    for r in range(len(vals) - 2, -1, -1):
        out = jnp.where(sub == r, vals[r], out)
    return out


def _sublane_total(x):
    shift = SUBLANES // 2
    while shift >= 1:
        x = x + pltpu.roll(x, shift, 0)
        shift //= 2
    return x


def _gelu_tanh(x):
    u = x * (0.7978845608028654 + 0.035677408136300125 * (x * x))
    return (0.5 * x) * (1.0 + jnp.tanh(u))


def _peer_select_head(s1, s2, tm):
    k, nslab = PEER_TOPK, PEER_NKEYS // SUBLANES
    half = SUBLANES
    x1 = [s1[g * SUBLANES:(g + 1) * SUBLANES, :] for g in range(nslab)]
    x2 = [s2[g * SUBLANES:(g + 1) * SUBLANES, :] for g in range(nslab)]
    t1 = _top_sorted(x1)
    t2 = _top_sorted(x2)
    sub = lax.broadcasted_iota(jnp.int32, (SUBLANES, tm), 0)
    v2lo, v2hi = _along_sublanes(t2[:half], sub), _along_sublanes(t2[half:], sub)
    v1hi = _along_sublanes(t1[half:], sub)
    cands = [t1[0] + v2lo, t1[0] + v2hi] + [t1[a] + v2lo for a in range(1, half)] + [v1hi + t2[0]]
    ninf = jnp.full((SUBLANES, tm), -jnp.inf, F32)
    tc = _top_sorted(cands + [ninf] * (k - len(cands)))
    thr, top = tc[k - 1], tc[0]
    z = cands[0] * 0.0
    for c in cands:
        z = z + jnp.where(c >= thr, jnp.exp(c - top), 0.0)
    inv_z = 1.0 / _sublane_total(z)
    cnt, rnk, e1, e2 = [], [], [], []
    for g in range(nslab):
        c = x1[g] * 0.0
        r = c
        for b in range(k):
            c = c + jnp.where(x1[g] + t2[b] >= thr, 1.0, 0.0)
            r = r + jnp.where(t2[b] > x2[g], 1.0, 0.0)
        cnt.append(c)
        rnk.append(r)
        e1.append(jnp.exp(x1[g] - t1[0]))
        e2.append(jnp.exp(x2[g] - t2[0]) * inv_z)
    cat = lambda parts: jnp.concatenate(parts, axis=0)
    return cat(cnt), cat(e1), cat(rnk), cat(e2)


def _peer_kernel(h_ref, gf_ref, wpq_ref, k1_ref, k2_ref, u_ref, vta_ref, vtb_ref, gfin_ref, y_ref,
                 xn_ref, cnt_ref, e1_ref, rnk_ref, e2_ref, a_ref, wa0_ref, wa1_ref, acc_ref, *, tm, hpb):
    step = pl.program_id(1)
    nsteps = pl.num_programs(1)
    nk = PEER_NKEYS

    @pl.when(step == 0)
    def _():
        h = h_ref[...]
        y = h * lax.rsqrt(jnp.mean(h * h, axis=-1, keepdims=True) + EPS)
        xn = (y * gf_ref[...]).astype(BF16)
        xn_ref[...] = xn
        acc_ref[...] = jnp.zeros(acc_ref.shape, F32)
        wa1_ref[...] = jnp.zeros(wa1_ref.shape, BF16)

        def per_head(hd, _):
            qh = _dot(xn_ref[...], wpq_ref[hd]).astype(BF16)
            s1 = _dot_nt(k1_ref[...], qh[:, :nk])
            s2 = _dot_nt(k2_ref[...], qh[:, nk:])
            cnt, e1, rnk, e2 = _peer_select_head(s1, s2, tm)
            cnt_ref[hd] = cnt
            e1_ref[hd] = e1
            rnk_ref[hd] = rnk.astype(BF16)
            e2_ref[hd] = e2.astype(BF16)
            return 0

        lax.fori_loop(0, PEER_HEADS, per_head, 0)

    def activations(pair):
        rows = slice(pair * 2 * nk, (pair + 1) * 2 * nk)
        a_ref[rows, :] = _dot_nt(u_ref[rows, :], xn_ref[...])

    def build(half, wa_ref):
        for jj in range(hpb):
            ii = half * hpb + jj
            i = step * (2 * hpb) + ii
            cnt_row = [cnt_ref[hd, pl.ds(i, 1), :] for hd in range(PEER_HEADS)]
            e1_row = [e1_ref[hd, pl.ds(i, 1), :] for hd in range(PEER_HEADS)]
            for lb in range(tm // LANES):
                lanes = slice(lb * LANES, (lb + 1) * LANES)
                cnt_b, e1_b = [], []
                for hd in range(PEER_HEADS):
                    shape = (PACKED_ROWS, LANES)
                    cnt_b.append(jnp.broadcast_to(cnt_row[hd][:, lanes], shape).astype(BF16))
                    e1_b.append(jnp.broadcast_to(e1_row[hd][:, lanes], shape).astype(BF16))
                for r in range(nk // PACKED_ROWS):
                    rs = slice(r * PACKED_ROWS, (r + 1) * PACKED_ROWS)
                    w = jnp.zeros((PACKED_ROWS, LANES), BF16)
                    for hd in range(PEER_HEADS):
                        keep = rnk_ref[hd, rs, lanes] < cnt_b[hd]
                        w = w + jnp.where(keep, e2_ref[hd, rs, lanes], jnp.zeros_like(w)) * e1_b[hd]
                    src = slice(ii * nk + r * PACKED_ROWS, ii * nk + (r + 1) * PACKED_ROWS)
                    dst = slice(jj * nk + r * PACKED_ROWS, jj * nk + (r + 1) * PACKED_ROWS)
                    wa_ref[dst, lanes] = w * _gelu_tanh(a_ref[src, lanes]).astype(BF16)

    @pl.when(step < nsteps - 1)
    def _():
        npair = hpb // 2
        activations(0)
        acc_ref[...] += _dot(vta_ref[...], wa1_ref[...])
        for pair in range(1, npair + 1):
            activations(pair)
        build(0, wa0_ref)
        acc_ref[...] += _dot(vtb_ref[...], wa0_ref[...])
        for pair in range(npair + 1, 2 * npair):
            activations(pair)
        build(1, wa1_ref)

    @pl.when(step == nsteps - 1)
    def _():
        acc_ref[...] += _dot(vta_ref[...], wa1_ref[...])
        hh = h_ref[...] + acc_ref[...].T
        y = hh * lax.rsqrt(jnp.mean(hh * hh, axis=-1, keepdims=True) + EPS)
        y_ref[...] = y * gfin_ref[...]


def peer(h, g_ffn, wpq_heads, k1_bf16, k2_bf16, u_bf16, vt_bf16, g_final, tm, hpb):
    n, d = h.shape
    nk = PEER_NKEYS
    assert hpb % 2 == 0 and nk % (2 * hpb) == 0 and tm % LANES == 0
    nblk = nk // (2 * hpb)
    nhalf = 2 * nblk
    full = lambda a: pl.BlockSpec(a.shape, lambda t, s: (0,) * a.ndim)
    gf = g_ffn.reshape(1, d)
    gfin = g_final.reshape(1, d)
    return pl.pallas_call(
        functools.partial(_peer_kernel, tm=tm, hpb=hpb),
        grid=(n // tm, nblk + 1),
        in_specs=[pl.BlockSpec((tm, d), lambda t, s: (t, 0)), full(gf), full(wpq_heads), full(k1_bf16), full(k2_bf16),
                  pl.BlockSpec((2 * hpb * nk, d), lambda t, s: (jnp.minimum(s, nblk - 1), 0)),
                  pl.BlockSpec((d, hpb * nk), lambda t, s: (0, jnp.maximum(2 * s - 1, 0))),
                  pl.BlockSpec((d, hpb * nk), lambda t, s: (0, jnp.minimum(2 * s, nhalf - 1))),
                  full(gfin)],
        out_specs=pl.BlockSpec((tm, d), lambda t, s: (t, 0)),
        out_shape=jax.ShapeDtypeStruct((n, d), F32),
        scratch_shapes=[
            pltpu.VMEM((tm, d), BF16),
            pltpu.VMEM((PEER_HEADS, nk, tm), F32), pltpu.VMEM((PEER_HEADS, nk, tm), F32),
            pltpu.VMEM((PEER_HEADS, nk, tm), BF16), pltpu.VMEM((PEER_HEADS, nk, tm), BF16),
            pltpu.VMEM((2 * hpb * nk, tm), F32),
            pltpu.VMEM((hpb * nk, tm), BF16), pltpu.VMEM((hpb * nk, tm), BF16),
            pltpu.VMEM((d, tm), F32),
        ],
        compiler_params=_cparams("parallel", "arbitrary"),
        name="peer",
    )(h, gf, wpq_heads, k1_bf16, k2_bf16, u_bf16, vt_bf16, vt_bf16, gfin)


IN_GROUP_WIDTHS = (4 * 512, 512, 512, 512, 512, 3 * 1024, LANES)
PEER_ROWS_PER_HALF_STEP = 4


def _prep_w_in(w_in):
    gq, gk, gv, gr = w_in[:, 0:512], w_in[:, 512:1024], w_in[:, 1024:1536], w_in[:, 1536:2048]
    glr = w_in[:, 2048:2064]
    fq, fk, fv = w_in[:, 2064:2576], w_in[:, 2576:3088], w_in[:, 3088:3600]
    ff = w_in[:, 3600:3608]
    mq = w_in[:, 3608:4120]
    gt = w_in[:, 4120:]
    small = jnp.concatenate([glr, ff, jnp.zeros((w_in.shape[0], LANES - GLA_RANK - FOX_HEADS), w_in.dtype)], axis=1)
    return jnp.concatenate([gq, gk, gv, gr, fq, fk, fv, mq, gt, small], axis=1).astype(BF16)


def _layer(x3, w, state, mk, mv, past):
    n_seq, t_len, d = x3.shape
    n = n_seq * t_len
    x = x3.reshape(n, d)
    tm = min(256, n)
    zg, fq, fk, fv, mq, gt, sm = rms_proj(x, w["g_mix"], w["w_in"], IN_GROUP_WIDTHS, tm)
    lf_full, d_full = fgate(sm, w["b_fgate_pad"], t_len)
    logf = lf_full[:, GLA_RANK:GLA_RANK + FOX_HEADS]
    o_gla, s_new = gla(zg, sm, w["w_a2"], w["b_a2"], state, n_seq, t_len)
    if past is None:
        dd = d_full[:, GLA_RANK:GLA_RANK + FOX_HEADS]
        tq = min(512, t_len)
        o_fox = fox_prompt(fq, fk, fv, dd, dd.T, n_seq, t_len, tq=tq, tk=tq)
        o_mem = mem_attn(mq, mk, mv, n_seq, t_len, mk.shape[0] // n_seq, tq=tq)
    else:
        cache_kt, cache_vt, cache_lft, page_table = past
        o_fox = fox_sample(fq, fk, fv, logf, cache_kt, cache_vt, cache_lft, page_table, t_len)
        o_mem = mem_attn(mq, mk, mv, n_seq, t_len, mk.shape[0] // n_seq, tq=t_len)
    h = merge(x, o_gla, zg, o_fox, o_mem, gt, w["b_gate"], w["g_gla_head"],
              w["w_gla_o"], w["w_fox_o"], w["w_mem_o"], w["w_out"], tm=min(512, n))
    y = peer(h, w["g_ffn"], w["w_pq"], w["peer_k1"], w["peer_k2"], w["peer_u"], w["peer_vt"], w["g_final"],
             tm=min(512, n), hpb=PEER_ROWS_PER_HALF_STEP)
    return y.reshape(n_seq, t_len, d), fk, fv, logf, s_new


def kernel(x_prompt, x_sample, cache_fox_k, cache_fox_v, cache_fox_logf, state_gla, cache_mem_k, cache_mem_v, page_table, mem_prompt, g_mix, w_in, w_a2, b_a2, b_fgate, b_gate, g_gla_head, w_gla_o, w_fox_o, w_mem_o, w_out, g_mem, w_mem_kv, g_ffn, w_pq, peer_k1, peer_k2, peer_u, peer_v, g_final):
    depth = w_in.shape[0]
    assert depth == 1
    l = 0
    bp, tp, d = x_prompt.shape
    bs, ts, _ = x_sample.shape
    nk = PEER_NKEYS
    w = {
        "g_mix": g_mix[l], "w_in": _prep_w_in(w_in[l]),
        "w_a2": jnp.zeros((LANES, w_a2.shape[2]), BF16).at[:GLA_RANK].set(w_a2[l].astype(BF16)), "b_a2": b_a2[l],
        "b_fgate_pad": jnp.zeros((1, LANES), F32).at[0, GLA_RANK:GLA_RANK + FOX_HEADS].set(b_fgate[l]),
        "b_gate": b_gate[l], "g_gla_head": g_gla_head[l],
        "w_gla_o": w_gla_o[l].astype(BF16), "w_fox_o": w_fox_o[l].astype(BF16),
        "w_mem_o": w_mem_o[l].astype(BF16), "w_out": w_out[l].astype(BF16),
        "g_ffn": g_ffn[l],
        "w_pq": w_pq[l].reshape(d, PEER_HEADS, 2 * nk).transpose(1, 0, 2).astype(BF16),
        "peer_k1": peer_k1[l].astype(BF16), "peer_k2": peer_k2[l].astype(BF16),
        "peer_u": peer_u[l].astype(BF16), "peer_vt": peer_v[l].astype(BF16).T,
        "g_final": g_final,
    }
    mem_len = mem_prompt.shape[1]
    mw = MEM_HEADS * MEM_HD
    mk_p, mv_p = rms_proj(mem_prompt.reshape(bp * mem_len, d), g_mem[l], w_mem_kv[l].astype(BF16), (mw, mw),
                          tm=256)
    s0_p = jnp.zeros((bp, GLA_HEADS, GLA_DK, GLA_DK), F32)
    y_p, kp, vp, lfp, sp = _layer(x_prompt, w, s0_p, mk_p, mv_p, None)
    past = (jnp.transpose(cache_fox_k[l], (0, 2, 3, 1)), jnp.transpose(cache_fox_v[l], (0, 2, 3, 1)),
            jnp.transpose(cache_fox_logf[l], (0, 2, 1)), page_table)
    y_s, ks, vs, lfs, ss = _layer(x_sample, w, state_gla[l], cache_mem_k[l].reshape(bs * mem_len, mw),
                                  cache_mem_v[l].reshape(bs * mem_len, mw), past)
    return (y_p, y_s,
            kp.reshape(1, bp, tp, FOX_HEADS, FOX_HD), vp.reshape(1, bp, tp, FOX_HEADS, FOX_HD),
            lfp.reshape(1, bp, tp, FOX_HEADS), sp[None],
            mk_p.reshape(1, bp, mem_len, MEM_HEADS, MEM_HD), mv_p.reshape(1, bp, mem_len, MEM_HEADS, MEM_HD),
            ks.reshape(1, bs, ts, FOX_HEADS, FOX_HD), vs.reshape(1, bs, ts, FOX_HEADS, FOX_HD),
            lfs.reshape(1, bs, ts, FOX_HEADS), ss[None])
```
